```python
import math
import jax, jax.numpy as jnp
from jax import lax
import numpy as np

D_MODEL = 1024
BATCH = 16
SEQ = 2048
DEPTH = 4

N_Q_HEADS = 8
N_KV_HEADS = 2
HEAD_DIM = 64
ATTN_WIDTH = N_Q_HEADS * HEAD_DIM
KV_WIDTH = N_KV_HEADS * HEAD_DIM
WINDOW = 128
ATTN_BLOCK = 128
GMLP_GROUPS = 4
GMLP_GROUP_DIM = 128
GMLP_WIDTH = GMLP_GROUPS * GMLP_GROUP_DIM
CHUNK = 128
SPLITS = (ATTN_WIDTH, KV_WIDTH, KV_WIDTH, ATTN_WIDTH,
          GMLP_WIDTH, GMLP_WIDTH, GMLP_WIDTH, D_MODEL, D_MODEL)
IN_WIDTH = sum(SPLITS)
ALPHA = (2 * DEPTH) ** 0.25
BETA = (8 * DEPTH) ** -0.25
LN_EPS = 1e-5

kernel_name = "hybrid_swa_sink_gmlp_deepnorm"


def layer_norm(x, g, b):
    xf = x.astype(jnp.float32)
    mu = xf.mean(-1, keepdims=True)
    var = jnp.square(xf - mu).mean(-1, keepdims=True)
    y = (xf - mu) * lax.rsqrt(var + LN_EPS)
    return (y * g.astype(jnp.float32) + b.astype(jnp.float32)).astype(x.dtype)


def alibi_slopes():
    h = jnp.arange(N_Q_HEADS, dtype=jnp.float32)
    return jnp.exp2(-8.0 * (h + 1.0) / N_Q_HEADS)


def split_columns(h):
    offs = [int(o) for o in np.cumsum(SPLITS)[:-1]]
    return jnp.split(h, offs, axis=-1)


def sliding_window_attention(q, k, v, sinks):
    B, S = q.shape[0], q.shape[1]
    nb = S // ATTN_BLOCK
    grp = N_Q_HEADS // N_KV_HEADS
    qb = q.reshape(B, nb, ATTN_BLOCK, N_KV_HEADS, grp, HEAD_DIM)

    def band(t):
        tb = t.reshape(B, nb, ATTN_BLOCK, N_KV_HEADS, HEAD_DIM)
        prev = jnp.pad(tb, ((0, 0), (1, 0), (0, 0), (0, 0), (0, 0)))[:, :-1]
        return jnp.concatenate([prev, tb], axis=2)

    kb, vb = band(k), band(v)
    scores = jnp.einsum('bnqhgd,bnkhd->bnhgqk', qb, kb).astype(jnp.float32) * (HEAD_DIM ** -0.5)
    qi = jnp.arange(ATTN_BLOCK)[:, None]
    kj = jnp.arange(2 * ATTN_BLOCK)[None, :]
    dist = qi + ATTN_BLOCK - kj
    blk = jnp.arange(nb)[:, None, None]
    key_pos = blk * ATTN_BLOCK + qi - dist
    valid = (dist >= 0) & (dist < WINDOW) & (key_pos >= 0)
    slopes = alibi_slopes().reshape(N_KV_HEADS, grp)
    scores = scores - slopes[:, :, None, None] * dist.astype(jnp.float32)
    scores = jnp.where(valid[None, :, None, None], scores, -jnp.inf)
    sink = sinks.astype(jnp.float32).reshape(N_KV_HEADS, grp)[None, None, :, :, None, None]
    m = jnp.maximum(scores.max(-1, keepdims=True), sink)
    p = jnp.exp(scores - m)
    denom = p.sum(-1, keepdims=True) + jnp.exp(sink - m)
    out = jnp.einsum('bnhgqk,bnkhd->bnqhgd', (p / denom).astype(v.dtype), vb)
    return out.reshape(B, S, ATTN_WIDTH)


def chunked_spatial_gating(u, v, ln_g, ln_b, w_s, b_s):
    B, S = v.shape[0], v.shape[1]
    nc = S // CHUNK
    vn = layer_norm(v, ln_g, ln_b).reshape(B, nc, CHUNK, GMLP_GROUPS, GMLP_GROUP_DIM)
    causal = jnp.tril(jnp.ones((CHUNK, CHUNK), dtype=bool))
    w = jnp.where(causal[None], w_s, jnp.zeros_like(w_s))
    mixed = jnp.einsum('gts,bcsgd->bctgd', w, vn) + b_s.T[None, None, :, :, None]
    return u * mixed.reshape(B, S, GMLP_WIDTH)


def setup_inputs(seed: int = 0) -> dict:
    key = jax.random.key(seed)
    ks = jax.random.split(key, 16)
    f32 = jnp.float32
    x = jax.random.normal(ks[0], (BATCH, SEQ, D_MODEL), f32)
    w_in = jax.random.normal(ks[1], (DEPTH, D_MODEL, IN_WIDTH), f32) * D_MODEL ** -0.5
    b_in = jax.random.normal(ks[2], (DEPTH, IN_WIDTH), f32) * 0.02
    attn_sinks = jax.random.normal(ks[3], (DEPTH, N_Q_HEADS), f32) * 0.5
    gmlp_ln_g = 1.0 + 0.02 * jax.random.normal(ks[4], (DEPTH, GMLP_WIDTH), f32)
    gmlp_ln_b = 0.02 * jax.random.normal(ks[5], (DEPTH, GMLP_WIDTH), f32)
    w_spatial = jax.random.normal(ks[6], (DEPTH, GMLP_GROUPS, CHUNK, CHUNK), f32) * (0.5 * CHUNK ** -0.5)
    b_spatial = 1.0 + 0.02 * jax.random.normal(ks[7], (DEPTH, GMLP_GROUPS, CHUNK), f32)
    w_branch_attn = jax.random.normal(ks[8], (DEPTH, ATTN_WIDTH, D_MODEL), f32) * (ATTN_WIDTH ** -0.5 * BETA)
    w_branch_gmlp = jax.random.normal(ks[9], (DEPTH, GMLP_WIDTH, D_MODEL), f32) * (GMLP_WIDTH ** -0.5 * BETA)
    w_out = jax.random.normal(ks[10], (DEPTH, D_MODEL, D_MODEL), f32) * (D_MODEL ** -0.5 * BETA)
    b_out = 0.02 * jax.random.normal(ks[11], (DEPTH, D_MODEL), f32)
    ln_g = 1.0 + 0.02 * jax.random.normal(ks[12], (DEPTH, D_MODEL), f32)
    ln_b = 0.02 * jax.random.normal(ks[13], (DEPTH, D_MODEL), f32)
    return {"x": x, "w_in": w_in, "b_in": b_in, "attn_sinks": attn_sinks,
            "gmlp_ln_g": gmlp_ln_g, "gmlp_ln_b": gmlp_ln_b,
            "w_spatial": w_spatial, "b_spatial": b_spatial,
            "w_branch_attn": w_branch_attn, "w_branch_gmlp": w_branch_gmlp,
            "w_out": w_out, "b_out": b_out, "ln_g": ln_g, "ln_b": ln_b}


def reference(x, w_in, b_in, attn_sinks, gmlp_ln_g, gmlp_ln_b, w_spatial, b_spatial,
              w_branch_attn, w_branch_gmlp, w_out, b_out, ln_g, ln_b):
    B, S = x.shape[0], x.shape[1]
    for l in range(DEPTH):
        h = jnp.einsum('bsd,de->bse', x, w_in[l]) + b_in[l]
        q, k, v, z_a, u_g, v_g, z_g, g_a, g_g = split_columns(h)
        y_a = sliding_window_attention(q.reshape(B, S, N_Q_HEADS, HEAD_DIM),
                                       k.reshape(B, S, N_KV_HEADS, HEAD_DIM),
                                       v.reshape(B, S, N_KV_HEADS, HEAD_DIM),
                                       attn_sinks[l]) * jax.nn.silu(z_a)
        y_g = chunked_spatial_gating(jax.nn.gelu(u_g, approximate=False),
                                     jax.nn.gelu(v_g, approximate=False),
                                     gmlp_ln_g[l], gmlp_ln_b[l],
                                     w_spatial[l], b_spatial[l]) * jax.nn.silu(z_g)
        br_a = jnp.einsum('bse,ed->bsd', y_a, w_branch_attn[l])
        br_g = jnp.einsum('bse,ed->bsd', y_g, w_branch_gmlp[l])
        merged = jax.nn.sigmoid(g_a) * br_a + jax.nn.sigmoid(g_g) * br_g
        out = jnp.einsum('bsd,de->bse', merged, w_out[l]) + b_out[l]
        x = layer_norm(ALPHA * x + out, ln_g[l], ln_b[l])
    return x
```

```python
import functools
import math

import jax
import jax.numpy as jnp
from jax import lax
from jax.experimental import pallas as pl
from jax.experimental.pallas import tpu as pltpu

D_MODEL = 1024
DEPTH = 4
N_Q_HEADS = 8
N_KV_HEADS = 2
HEAD_DIM = 64
GROUP = N_Q_HEADS // N_KV_HEADS
ATTN_WIDTH = N_Q_HEADS * HEAD_DIM
KV_WIDTH = N_KV_HEADS * HEAD_DIM
WINDOW = 128
ATTN_BLOCK = 128
GMLP_GROUPS = 4
GMLP_GROUP_DIM = 128
GMLP_WIDTH = GMLP_GROUPS * GMLP_GROUP_DIM
CHUNK = 128
SPLITS = (ATTN_WIDTH, KV_WIDTH, KV_WIDTH, ATTN_WIDTH,
          GMLP_WIDTH, GMLP_WIDTH, GMLP_WIDTH, D_MODEL, D_MODEL)
IN_WIDTH = sum(SPLITS)
ALPHA = (2 * DEPTH) ** 0.25
LN_EPS = 1e-5

_OFFS = [0]
for _w in SPLITS:
    _OFFS.append(_OFFS[-1] + _w)
(Q0, K0, V0, ZA0, U0, VG0, ZG0, GA0, GG0, _END) = _OFFS

TOKENS_PER_STEP = 512
BLOCKS_PER_STEP = TOKENS_PER_STEP // ATTN_BLOCK
VMEM_LIMIT_BYTES = 56 * 1024 * 1024

BF16 = jnp.bfloat16
F32 = jnp.float32


def _dot(a, b):
    return jnp.dot(a, b, preferred_element_type=F32)


def _silu(x):
    return x * jax.nn.sigmoid(x)


def _gelu(x):
    return 0.5 * x * (1.0 + lax.erf(x * math.sqrt(0.5)))


def _build_bias(bias_ref):
    qi = lax.broadcasted_iota(jnp.int32, (ATTN_BLOCK, 2 * ATTN_BLOCK), 0)
    kj = lax.broadcasted_iota(jnp.int32, (ATTN_BLOCK, 2 * ATTN_BLOCK), 1)
    dist = qi + ATTN_BLOCK - kj
    valid = (dist >= 0) & (dist < WINDOW)
    valid_first = valid & (kj >= ATTN_BLOCK)
    distf = dist.astype(F32)
    for h in range(N_Q_HEADS):
        slope = 2.0 ** (-8.0 * (h + 1.0) / N_Q_HEADS)
        hk, g = divmod(h, GROUP)
        rows = slice(g * ATTN_BLOCK, (g + 1) * ATTN_BLOCK)
        b = -slope * distf
        bias_ref[0, hk, rows, :] = jnp.where(valid, b, -jnp.inf)
        bias_ref[1, hk, rows, :] = jnp.where(valid_first, b, -jnp.inf)


def _layer_norm(y, g, b):
    mu = jnp.mean(y, axis=-1, keepdims=True)
    yc = y - mu
    var = jnp.mean(yc * yc, axis=-1, keepdims=True)
    return yc * lax.rsqrt(var + LN_EPS) * g + b


def _layer_kernel(sinks_ref, x_ref, w_in_ref, b_in_ref, gln_g_ref, gln_b_ref, ws_ref, bs_ref,
                  w_ba_ref, w_bg_ref, w_out_ref, b_out_ref, ln_g_ref, ln_b_ref,
                  o_ref, kv_ref, bias_ref):
    b_idx = pl.program_id(0)
    j = pl.program_id(1)

    @pl.when((b_idx == 0) & (j == 0))
    def _():
        _build_bias(bias_ref)

    @pl.when(j == 0)
    def _():
        kv_ref[...] = jnp.zeros_like(kv_ref)

    x = x_ref[...]
    xb = x.astype(BF16)

    def proj(lo, hi):
        return _dot(xb, w_in_ref[:, lo:hi]) + b_in_ref[:, lo:hi]

    q = (proj(Q0, K0) * (HEAD_DIM ** -0.5)).astype(BF16)
    kv = proj(K0, ZA0).astype(BF16)
    kv_ext = jnp.concatenate([kv_ref[...], kv], axis=0)
    kv_ref[...] = kv[TOKENS_PER_STEP - ATTN_BLOCK:, :]

    row_group = lax.broadcasted_iota(jnp.int32, (GROUP * ATTN_BLOCK, 1), 0) // ATTN_BLOCK
    first_sel = jnp.where(j == 0, 1, 0)

    head_out = [[None] * N_Q_HEADS for _ in range(BLOCKS_PER_STEP)]
    for hk in range(N_KV_HEADS):
        k_h = kv_ext[:, hk * HEAD_DIM:(hk + 1) * HEAD_DIM]
        v_h = kv_ext[:, KV_WIDTH + hk * HEAD_DIM:KV_WIDTH + (hk + 1) * HEAD_DIM]
        sink_col = jnp.zeros((GROUP * ATTN_BLOCK, 1), F32)
        for g in range(GROUP):
            sink_col = jnp.where(row_group == g, sinks_ref[hk * GROUP + g], sink_col)
        for i in range(BLOCKS_PER_STEP):
            r0 = i * ATTN_BLOCK
            qs = jnp.concatenate(
                [q[r0:r0 + ATTN_BLOCK, (hk * GROUP + g) * HEAD_DIM:(hk * GROUP + g + 1) * HEAD_DIM]
                 for g in range(GROUP)], axis=0)
            kk = k_h[r0:r0 + 2 * ATTN_BLOCK, :]
            vv = v_h[r0:r0 + 2 * ATTN_BLOCK, :]
            s = lax.dot_general(qs, kk, (((1,), (1,)), ((), ())), preferred_element_type=F32)
            if i == 0:
                s = s + bias_ref[first_sel, hk]
            else:
                s = s + bias_ref[0, hk]
            m = jnp.maximum(jnp.max(s, axis=-1, keepdims=True), sink_col)
            p = jnp.exp(s - m)
            denom = jnp.sum(p, axis=-1, keepdims=True) + jnp.exp(sink_col - m)
            o = _dot(p.astype(BF16), vv) / denom
            for g in range(GROUP):
                head_out[i][hk * GROUP + g] = o[g * ATTN_BLOCK:(g + 1) * ATTN_BLOCK, :]
    att = jnp.concatenate([jnp.concatenate(head_out[i], axis=1) for i in range(BLOCKS_PER_STEP)],
                          axis=0)

    y_a = (att * _silu(proj(ZA0, U0))).astype(BF16)
    merged = jax.nn.sigmoid(proj(GA0, GG0)) * _dot(y_a, w_ba_ref[...])

    u = _gelu(proj(U0, VG0))
    v = _gelu(proj(VG0, ZG0))
    vn = _layer_norm(v, gln_g_ref[...], gln_b_ref[...]).astype(BF16)
    causal = (lax.broadcasted_iota(jnp.int32, (CHUNK, CHUNK), 0)
              >= lax.broadcasted_iota(jnp.int32, (CHUNK, CHUNK), 1))
    n_chunks = TOKENS_PER_STEP // CHUNK
    mixed_cols = []
    for grp in range(GMLP_GROUPS):
        c0 = grp * GMLP_GROUP_DIM
        w = jnp.where(causal, ws_ref[grp], 0.0).astype(BF16)
        rhs = jnp.concatenate([vn[c * CHUNK:(c + 1) * CHUNK, c0:c0 + GMLP_GROUP_DIM]
                               for c in range(n_chunks)], axis=1)
        mix = _dot(w, rhs) + bs_ref[grp]
        mixed_cols.append(jnp.concatenate(
            [mix[:, c * GMLP_GROUP_DIM:(c + 1) * GMLP_GROUP_DIM] for c in range(n_chunks)], axis=0))
    mixed = jnp.concatenate(mixed_cols, axis=1)
    y_g = (u * mixed * _silu(proj(ZG0, GA0))).astype(BF16)
    merged = merged + jax.nn.sigmoid(proj(GG0, _END)) * _dot(y_g, w_bg_ref[...])

    out = _dot(merged.astype(BF16), w_out_ref[...]) + b_out_ref[...]
    o_ref[...] = _layer_norm(ALPHA * x + out, ln_g_ref[...], ln_b_ref[...])


def _const_spec(shape, layer):
    zeros = (0,) * len(shape)
    return pl.BlockSpec((None,) + tuple(shape), lambda b, j: (layer,) + zeros,
                        pipeline_mode=pl.Buffered(1))


def _layer_call(layer, x, params):
    batch, seq, _ = x.shape
    assert seq % TOKENS_PER_STEP == 0
    (sinks, w_in, b_in, gln_g, gln_b, ws, bs, w_ba, w_bg, w_out, b_out, ln_g, ln_b) = params
    x_spec = pl.BlockSpec((None, TOKENS_PER_STEP, D_MODEL), lambda b, j: (b, j, 0))
    in_specs = [
        pl.BlockSpec(memory_space=pltpu.SMEM),
        x_spec,
        _const_spec((D_MODEL, IN_WIDTH), layer),
        _const_spec((1, IN_WIDTH), layer),
        _const_spec((1, GMLP_WIDTH), layer),
        _const_spec((1, GMLP_WIDTH), layer),
        _const_spec((GMLP_GROUPS, CHUNK, CHUNK), layer),
        _const_spec((GMLP_GROUPS, CHUNK, 1), layer),
        _const_spec((ATTN_WIDTH, D_MODEL), layer),
        _const_spec((GMLP_WIDTH, D_MODEL), layer),
        _const_spec((D_MODEL, D_MODEL), layer),
        _const_spec((1, D_MODEL), layer),
        _const_spec((1, D_MODEL), layer),
        _const_spec((1, D_MODEL), layer),
    ]
    return pl.pallas_call(
        _layer_kernel,
        grid=(batch, seq // TOKENS_PER_STEP),
        in_specs=in_specs,
        out_specs=x_spec,
        out_shape=jax.ShapeDtypeStruct(x.shape, x.dtype),
        scratch_shapes=[
            pltpu.VMEM((ATTN_BLOCK, 2 * KV_WIDTH), BF16),
            pltpu.VMEM((2, N_KV_HEADS, GROUP * ATTN_BLOCK, 2 * ATTN_BLOCK), F32),
        ],
        compiler_params=pltpu.CompilerParams(
            dimension_semantics=("arbitrary", "arbitrary"),
            vmem_limit_bytes=VMEM_LIMIT_BYTES),
        name=f"trunk_layer_{layer}",
    )(sinks[layer], x, w_in, b_in, gln_g, gln_b, ws, bs, w_ba, w_bg, w_out, b_out, ln_g, ln_b)


def kernel(x, w_in, b_in, attn_sinks, gmlp_ln_g, gmlp_ln_b, w_spatial, b_spatial,
           w_branch_attn, w_branch_gmlp, w_out, b_out, ln_g, ln_b):
    params = (
        attn_sinks,
        w_in.astype(BF16),
        b_in[:, None, :],
        gmlp_ln_g[:, None, :],
        gmlp_ln_b[:, None, :],
        w_spatial,
        b_spatial[..., None],
        w_branch_attn.astype(BF16),
        w_branch_gmlp.astype(BF16),
        w_out.astype(BF16),
        b_out[:, None, :],
        ln_g[:, None, :],
        ln_b[:, None, :],
    )
    for layer in range(DEPTH):
        x = _layer_call(layer, x, params)
    return x
```

```python
import math

import jax
import jax.numpy as jnp
from jax import lax
from jax.experimental import pallas as pl
from jax.experimental.pallas import tpu as pltpu

D_MODEL = 1024
DEPTH = 4
N_Q_HEADS = 8
N_KV_HEADS = 2
HEAD_DIM = 64
GROUP = N_Q_HEADS // N_KV_HEADS
ATTN_WIDTH = N_Q_HEADS * HEAD_DIM
KV_WIDTH = N_KV_HEADS * HEAD_DIM
WINDOW = 128
ATTN_BLOCK = 128
GMLP_GROUPS = 4
GMLP_GROUP_DIM = 128
GMLP_WIDTH = GMLP_GROUPS * GMLP_GROUP_DIM
CHUNK = 128
SPLITS = (ATTN_WIDTH, KV_WIDTH, KV_WIDTH, ATTN_WIDTH,
          GMLP_WIDTH, GMLP_WIDTH, GMLP_WIDTH, D_MODEL, D_MODEL)
IN_WIDTH = sum(SPLITS)
ALPHA = (2 * DEPTH) ** 0.25
LN_EPS = 1e-5

_OFFS = [0]
for _w in SPLITS:
    _OFFS.append(_OFFS[-1] + _w)
(Q0, K0, V0, ZA0, U0, VG0, ZG0, GA0, GG0, _END) = _OFFS

TOKENS_PER_STEP = 512
BLOCKS_PER_STEP = TOKENS_PER_STEP // ATTN_BLOCK
CHUNKS_PER_STEP = TOKENS_PER_STEP // CHUNK
VMEM_LIMIT_BYTES = 56 * 1024 * 1024

BF16 = jnp.bfloat16
F32 = jnp.float32


def _silu(x):
    return x * jax.nn.sigmoid(x)


def _gelu(x):
    return 0.5 * x * (1.0 + lax.erf(x * math.sqrt(0.5)))


def _build_bias(bias_ref):
    qi = lax.broadcasted_iota(jnp.int32, (ATTN_BLOCK, 2 * ATTN_BLOCK), 0)
    kj = lax.broadcasted_iota(jnp.int32, (ATTN_BLOCK, 2 * ATTN_BLOCK), 1)
    dist = qi + ATTN_BLOCK - kj
    valid = (dist >= 0) & (dist < WINDOW)
    valid_first = valid & (kj >= ATTN_BLOCK)
    distf = dist.astype(F32)
    for h in range(N_Q_HEADS):
        slope = 2.0 ** (-8.0 * (h + 1.0) / N_Q_HEADS)
        hk, g = divmod(h, GROUP)
        rows = slice(g * ATTN_BLOCK, (g + 1) * ATTN_BLOCK)
        b = -slope * distf
        bias_ref[0, hk, rows, :] = jnp.where(valid, b, -jnp.inf)
        bias_ref[1, hk, rows, :] = jnp.where(valid_first, b, -jnp.inf)


def _layer_norm(y, g, b):
    mu = jnp.mean(y, axis=-1, keepdims=True)
    yc = y - mu
    var = jnp.mean(yc * yc, axis=-1, keepdims=True)
    return yc * lax.rsqrt(var + LN_EPS) * g + b


def _layer_kernel(sinks_ref, x_ref, w_in_ref, b_in_ref, gln_g_ref, gln_b_ref, ws_ref, bs_ref,
                  w_ba_ref, w_bg_ref, w_out_ref, b_out_ref, ln_g_ref, ln_b_ref,
                  o_ref, kv_ref, bias_ref):
    b_idx = pl.program_id(0)
    j = pl.program_id(1)

    @pl.when((b_idx == 0) & (j == 0))
    def _():
        _build_bias(bias_ref)

    read_slot = j % 2
    write_slot = 1 - read_slot

    @pl.when(j == 0)
    def _():
        kv_ref[0] = jnp.zeros(kv_ref.shape[1:], kv_ref.dtype)

    x = x_ref[...]
    xb = x.astype(BF16)

    h_q = jnp.dot(xb, w_in_ref[:, Q0:K0], preferred_element_type=F32) + b_in_ref[:, Q0:K0]
    h_kv = jnp.dot(xb, w_in_ref[:, K0:ZA0], preferred_element_type=F32) + b_in_ref[:, K0:ZA0]
    h_vg = jnp.dot(xb, w_in_ref[:, VG0:ZG0], preferred_element_type=F32) + b_in_ref[:, VG0:ZG0]

    q = (h_q * (HEAD_DIM ** -0.5)).astype(BF16)
    kv = h_kv.astype(BF16)
    kv_ext = jnp.concatenate([kv_ref[read_slot], kv], axis=0)
    kv_ref[write_slot] = kv[TOKENS_PER_STEP - ATTN_BLOCK:, :]

    first_sel = jnp.where(j == 0, 1, 0)
    units = [(hk, i) for hk in range(N_KV_HEADS) for i in range(BLOCKS_PER_STEP)]
    k_heads = [kv_ext[:, hk * HEAD_DIM:(hk + 1) * HEAD_DIM] for hk in range(N_KV_HEADS)]
    v_heads = [kv_ext[:, KV_WIDTH + hk * HEAD_DIM:KV_WIDTH + (hk + 1) * HEAD_DIM]
               for hk in range(N_KV_HEADS)]
    scores = []
    for hk, i in units:
        r0 = i * ATTN_BLOCK
        qs = jnp.concatenate(
            [q[r0:r0 + ATTN_BLOCK, (hk * GROUP + g) * HEAD_DIM:(hk * GROUP + g + 1) * HEAD_DIM]
             for g in range(GROUP)], axis=0)
        kk = k_heads[hk][r0:r0 + 2 * ATTN_BLOCK, :]
        scores.append(lax.dot_general(qs, kk, (((1,), (1,)), ((), ())),
                                      preferred_element_type=F32))

    h_u = jnp.dot(xb, w_in_ref[:, U0:VG0], preferred_element_type=F32) + b_in_ref[:, U0:VG0]
    h_zg = jnp.dot(xb, w_in_ref[:, ZG0:GA0], preferred_element_type=F32) + b_in_ref[:, ZG0:GA0]

    row_group = lax.broadcasted_iota(jnp.int32, (GROUP * ATTN_BLOCK, 1), 0) // ATTN_BLOCK
    sink_cols = []
    for hk in range(N_KV_HEADS):
        sink_col = jnp.zeros((GROUP * ATTN_BLOCK, 1), F32)
        for g in range(GROUP):
            sink_col = jnp.where(row_group == g, sinks_ref[hk * GROUP + g], sink_col)
        sink_cols.append(sink_col)
    probs, denoms = [], []
    for (hk, i), s in zip(units, scores):
        s = s + (bias_ref[first_sel, hk] if i == 0 else bias_ref[0, hk])
        m = jnp.maximum(jnp.max(s, axis=-1, keepdims=True), sink_cols[hk])
        p = jnp.exp(s - m)
        denoms.append(jnp.sum(p, axis=-1, keepdims=True) + jnp.exp(sink_cols[hk] - m))
        probs.append(p.astype(BF16))

    h_za = jnp.dot(xb, w_in_ref[:, ZA0:U0], preferred_element_type=F32) + b_in_ref[:, ZA0:U0]

    head_out = [[None] * N_Q_HEADS for _ in range(BLOCKS_PER_STEP)]
    for (hk, i), p, denom in zip(units, probs, denoms):
        r0 = i * ATTN_BLOCK
        vv = v_heads[hk][r0:r0 + 2 * ATTN_BLOCK, :]
        o = jnp.dot(p, vv, preferred_element_type=F32) / denom
        for g in range(GROUP):
            head_out[i][hk * GROUP + g] = o[g * ATTN_BLOCK:(g + 1) * ATTN_BLOCK, :]

    h_ga = jnp.dot(xb, w_in_ref[:, GA0:GG0], preferred_element_type=F32) + b_in_ref[:, GA0:GG0]

    vn = _layer_norm(_gelu(h_vg), gln_g_ref[...], gln_b_ref[...]).astype(BF16)
    causal = (lax.broadcasted_iota(jnp.int32, (CHUNK, CHUNK), 0)
              >= lax.broadcasted_iota(jnp.int32, (CHUNK, CHUNK), 1))
    mixed_cols = []
    for grp in range(GMLP_GROUPS):
        c0 = grp * GMLP_GROUP_DIM
        w = jnp.where(causal, ws_ref[grp], 0.0).astype(BF16)
        rhs = jnp.concatenate([vn[c * CHUNK:(c + 1) * CHUNK, c0:c0 + GMLP_GROUP_DIM]
                               for c in range(CHUNKS_PER_STEP)], axis=1)
        mix = jnp.dot(w, rhs, preferred_element_type=F32) + bs_ref[grp]
        mixed_cols.append(jnp.concatenate(
            [mix[:, c * GMLP_GROUP_DIM:(c + 1) * GMLP_GROUP_DIM] for c in range(CHUNKS_PER_STEP)],
            axis=0))
    mixed = jnp.concatenate(mixed_cols, axis=1)

    h_gg = jnp.dot(xb, w_in_ref[:, GG0:_END], preferred_element_type=F32) + b_in_ref[:, GG0:_END]

    att = jnp.concatenate([jnp.concatenate(head_out[i], axis=1) for i in range(BLOCKS_PER_STEP)],
                          axis=0)
    y_a = (att * _silu(h_za)).astype(BF16)
    y_g = (_gelu(h_u) * mixed * _silu(h_zg)).astype(BF16)

    br_a = jnp.dot(y_a, w_ba_ref[...], preferred_element_type=F32)
    br_g = jnp.dot(y_g, w_bg_ref[...], preferred_element_type=F32)
    merged = jax.nn.sigmoid(h_ga) * br_a + jax.nn.sigmoid(h_gg) * br_g
    out = jnp.dot(merged.astype(BF16), w_out_ref[...], preferred_element_type=F32) + b_out_ref[...]
    o_ref[...] = _layer_norm(ALPHA * x + out, ln_g_ref[...], ln_b_ref[...])


def _const_spec(shape, layer):
    zeros = (0,) * len(shape)
    return pl.BlockSpec((None,) + tuple(shape), lambda b, j: (layer,) + zeros,
                        pipeline_mode=pl.Buffered(1))


def _layer_call(layer, x, params):
    batch, seq, _ = x.shape
    assert seq % TOKENS_PER_STEP == 0
    (sinks, w_in, b_in, gln_g, gln_b, ws, bs, w_ba, w_bg, w_out, b_out, ln_g, ln_b) = params
    x_spec = pl.BlockSpec((None, TOKENS_PER_STEP, D_MODEL), lambda b, j: (b, j, 0))
    in_specs = [
        pl.BlockSpec(memory_space=pltpu.SMEM),
        x_spec,
        _const_spec((D_MODEL, IN_WIDTH), layer),
        _const_spec((1, IN_WIDTH), layer),
        _const_spec((1, GMLP_WIDTH), layer),
        _const_spec((1, GMLP_WIDTH), layer),
        _const_spec((GMLP_GROUPS, CHUNK, CHUNK), layer),
        _const_spec((GMLP_GROUPS, CHUNK, 1), layer),
        _const_spec((ATTN_WIDTH, D_MODEL), layer),
        _const_spec((GMLP_WIDTH, D_MODEL), layer),
        _const_spec((D_MODEL, D_MODEL), layer),
        _const_spec((1, D_MODEL), layer),
        _const_spec((1, D_MODEL), layer),
        _const_spec((1, D_MODEL), layer),
    ]
    return pl.pallas_call(
        _layer_kernel,
        grid=(batch, seq // TOKENS_PER_STEP),
        in_specs=in_specs,
        out_specs=x_spec,
        out_shape=jax.ShapeDtypeStruct(x.shape, x.dtype),
        scratch_shapes=[
            pltpu.VMEM((2, ATTN_BLOCK, 2 * KV_WIDTH), BF16),
            pltpu.VMEM((2, N_KV_HEADS, GROUP * ATTN_BLOCK, 2 * ATTN_BLOCK), F32),
        ],
        compiler_params=pltpu.CompilerParams(
            dimension_semantics=("arbitrary", "arbitrary"),
            vmem_limit_bytes=VMEM_LIMIT_BYTES),
        name=f"trunk_layer_{layer}",
    )(sinks[layer], x, w_in, b_in, gln_g, gln_b, ws, bs, w_ba, w_bg, w_out, b_out, ln_g, ln_b)


def kernel(x, w_in, b_in, attn_sinks, gmlp_ln_g, gmlp_ln_b, w_spatial, b_spatial,
           w_branch_attn, w_branch_gmlp, w_out, b_out, ln_g, ln_b):
    params = (
        attn_sinks,
        w_in.astype(BF16),
        b_in[:, None, :],
        gmlp_ln_g[:, None, :],
        gmlp_ln_b[:, None, :],
        w_spatial,
        b_spatial[..., None],
        w_branch_attn.astype(BF16),
        w_branch_gmlp.astype(BF16),
        w_out.astype(BF16),
        b_out[:, None, :],
        ln_g[:, None, :],
        ln_b[:, None, :],
    )
    for layer in range(DEPTH):
        x = _layer_call(layer, x, params)
    return x
```

```python
import functools
import math

import jax
import jax.numpy as jnp
from jax import lax
from jax.experimental import pallas as pl
from jax.experimental.pallas import tpu as pltpu

D_MODEL = 1024
DEPTH = 4
N_Q_HEADS = 8
N_KV_HEADS = 2
HEAD_DIM = 64
GROUP = N_Q_HEADS // N_KV_HEADS
ATTN_WIDTH = N_Q_HEADS * HEAD_DIM
KV_WIDTH = N_KV_HEADS * HEAD_DIM
WINDOW = 128
ATTN_BLOCK = 128
GMLP_GROUPS = 4
GMLP_GROUP_DIM = 128
GMLP_WIDTH = GMLP_GROUPS * GMLP_GROUP_DIM
CHUNK = 128
SPLITS = (ATTN_WIDTH, KV_WIDTH, KV_WIDTH, ATTN_WIDTH,
          GMLP_WIDTH, GMLP_WIDTH, GMLP_WIDTH, D_MODEL, D_MODEL)
IN_WIDTH = sum(SPLITS)
ALPHA = (2 * DEPTH) ** 0.25
LN_EPS = 1e-5

_OFFS = [0]
for _w in SPLITS:
    _OFFS.append(_OFFS[-1] + _w)
(Q0, K0, V0, ZA0, U0, VG0, ZG0, GA0, GG0, _END) = _OFFS
QKV_WIDTH = ZA0 - Q0
AHEAD_VG = 256
AHEAD_WIDTH = QKV_WIDTH + AHEAD_VG

TOKENS_PER_STEP = 512
BLOCKS_PER_STEP = TOKENS_PER_STEP // ATTN_BLOCK
CHUNKS_PER_STEP = TOKENS_PER_STEP // CHUNK
VMEM_LIMIT_BYTES = 56 * 1024 * 1024

BF16 = jnp.bfloat16
F32 = jnp.float32


def _silu(x):
    return x * jax.nn.sigmoid(x)


def _gelu(x):
    return 0.5 * x * (1.0 + lax.erf(x * math.sqrt(0.5)))


def _build_bias(bias_ref):
    qi = lax.broadcasted_iota(jnp.int32, (ATTN_BLOCK, 2 * ATTN_BLOCK), 0)
    kj = lax.broadcasted_iota(jnp.int32, (ATTN_BLOCK, 2 * ATTN_BLOCK), 1)
    dist = qi + ATTN_BLOCK - kj
    valid = (dist >= 0) & (dist < WINDOW)
    valid_first = valid & (kj >= ATTN_BLOCK)
    distf = dist.astype(F32)
    for h in range(N_Q_HEADS):
        slope = 2.0 ** (-8.0 * (h + 1.0) / N_Q_HEADS)
        hk, g = divmod(h, GROUP)
        rows = slice(g * ATTN_BLOCK, (g + 1) * ATTN_BLOCK)
        b = -slope * distf
        bias_ref[0, hk, rows, :] = jnp.where(valid, b, -jnp.inf)
        bias_ref[1, hk, rows, :] = jnp.where(valid_first, b, -jnp.inf)


def _layer_norm(y, g, b):
    mu = jnp.mean(y, axis=-1, keepdims=True)
    yc = y - mu
    var = jnp.mean(yc * yc, axis=-1, keepdims=True)
    return yc * lax.rsqrt(var + LN_EPS) * g + b


def _project_ahead(xb, w_in_ref, b_in_ref):
    h_qkv = jnp.dot(xb, w_in_ref[:, Q0:ZA0], preferred_element_type=F32) + b_in_ref[:, Q0:ZA0]
    h_vg = (jnp.dot(xb, w_in_ref[:, VG0:VG0 + AHEAD_VG], preferred_element_type=F32)
            + b_in_ref[:, VG0:VG0 + AHEAD_VG])
    return jnp.concatenate([h_qkv, h_vg], axis=1)


def _layer_kernel(sinks_ref, x_ref, x_next_ref, w_in_ref, b_in_ref, gln_g_ref, gln_b_ref,
                  ws_ref, bs_ref, w_ba_ref, w_bg_ref, w_out_ref, b_out_ref, ln_g_ref, ln_b_ref,
                  o_ref, kv_ref, bias_ref, ahead_ref, xb_ref, *, tiles_per_seq):
    t = pl.program_id(0)
    j = t % tiles_per_seq
    read_slot = t % 2
    write_slot = 1 - read_slot

    @pl.when(t == 0)
    def _():
        _build_bias(bias_ref)
        xb0 = x_ref[...].astype(BF16)
        xb_ref[0] = xb0
        ahead_ref[0] = _project_ahead(xb0, w_in_ref, b_in_ref)

    @pl.when(j == 0)
    def _():
        kv_ref[read_slot] = jnp.zeros(kv_ref.shape[1:], kv_ref.dtype)

    x = x_ref[...]

    def proj(lo, hi):
        return (jnp.dot(xb_ref[read_slot], w_in_ref[:, lo:hi], preferred_element_type=F32)
                + b_in_ref[:, lo:hi])

    ahead = ahead_ref[read_slot]
    q = (ahead[:, :ATTN_WIDTH] * (HEAD_DIM ** -0.5)).astype(BF16)
    kv = ahead[:, ATTN_WIDTH:QKV_WIDTH].astype(BF16)
    kv_ext = jnp.concatenate([kv_ref[read_slot], kv], axis=0)
    kv_ref[write_slot] = kv[TOKENS_PER_STEP - ATTN_BLOCK:, :]

    first_sel = jnp.where(j == 0, 1, 0)
    k_heads = [kv_ext[:, hk * HEAD_DIM:(hk + 1) * HEAD_DIM] for hk in range(N_KV_HEADS)]
    v_heads = [kv_ext[:, KV_WIDTH + hk * HEAD_DIM:KV_WIDTH + (hk + 1) * HEAD_DIM]
               for hk in range(N_KV_HEADS)]
    row_group = lax.broadcasted_iota(jnp.int32, (GROUP * ATTN_BLOCK, 1), 0) // ATTN_BLOCK
    sink_cols = []
    for hk in range(N_KV_HEADS):
        sink_col = jnp.zeros((GROUP * ATTN_BLOCK, 1), F32)
        for g in range(GROUP):
            sink_col = jnp.where(row_group == g, sinks_ref[hk * GROUP + g], sink_col)
        sink_cols.append(sink_col)

    def attn_probs(hk, i):
        r0 = i * ATTN_BLOCK
        qs = jnp.concatenate(
            [q[r0:r0 + ATTN_BLOCK, (hk * GROUP + g) * HEAD_DIM:(hk * GROUP + g + 1) * HEAD_DIM]
             for g in range(GROUP)], axis=0)
        kk = k_heads[hk][r0:r0 + 2 * ATTN_BLOCK, :]
        s = lax.dot_general(qs, kk, (((1,), (1,)), ((), ())), preferred_element_type=F32)
        s = s + (bias_ref[first_sel, hk] if i == 0 else bias_ref[0, hk])
        m = jnp.maximum(jnp.max(s, axis=-1, keepdims=True), sink_cols[hk])
        p = jnp.exp(s - m)
        denom = jnp.sum(p, axis=-1, keepdims=True) + jnp.exp(sink_cols[hk] - m)
        return p.astype(BF16), denom

    head_out = [[None] * N_Q_HEADS for _ in range(BLOCKS_PER_STEP)]

    def attn_values(hk, i, p, denom):
        r0 = i * ATTN_BLOCK
        vv = v_heads[hk][r0:r0 + 2 * ATTN_BLOCK, :]
        o = jnp.dot(p, vv, preferred_element_type=F32) / denom
        for g in range(GROUP):
            head_out[i][hk * GROUP + g] = o[g * ATTN_BLOCK:(g + 1) * ATTN_BLOCK, :]

    pairs = [[(hk, i) for hk in range(N_KV_HEADS)] for i in range(BLOCKS_PER_STEP)]
    ga_mid = (GA0 + GG0) // 2
    projections = iter([(VG0 + AHEAD_VG, ZG0), (U0, VG0), (ZG0, GA0), (ZA0, U0)])
    proj_out = {}
    pending = None
    for pair in pairs:
        started = [(hk, i) + attn_probs(hk, i) for hk, i in pair]
        if pending is not None:
            for unit in pending:
                attn_values(*unit)
        lo, hi = next(projections)
        proj_out[lo] = proj(lo, hi)
        pending = started
    proj_out[GA0] = proj(GA0, ga_mid)
    for unit in pending:
        attn_values(*unit)
    h_vg = jnp.concatenate([ahead[:, QKV_WIDTH:], proj_out[VG0 + AHEAD_VG]], axis=1)
    h_u, h_zg, h_za = proj_out[U0], proj_out[ZG0], proj_out[ZA0]

    h_ga = jnp.concatenate([proj_out[GA0], proj(ga_mid, GG0)], axis=1)

    vn = _layer_norm(_gelu(h_vg), gln_g_ref[...], gln_b_ref[...]).astype(BF16)
    causal = (lax.broadcasted_iota(jnp.int32, (CHUNK, CHUNK), 0)
              >= lax.broadcasted_iota(jnp.int32, (CHUNK, CHUNK), 1))
    mixed_cols = []
    for grp in range(GMLP_GROUPS):
        c0 = grp * GMLP_GROUP_DIM
        w = jnp.where(causal, ws_ref[grp], 0.0).astype(BF16)
        rhs = jnp.concatenate([vn[c * CHUNK:(c + 1) * CHUNK, c0:c0 + GMLP_GROUP_DIM]
                               for c in range(CHUNKS_PER_STEP)], axis=1)
        mix = jnp.dot(w, rhs, preferred_element_type=F32) + bs_ref[grp]
        mixed_cols.append(jnp.concatenate(
            [mix[:, c * GMLP_GROUP_DIM:(c + 1) * GMLP_GROUP_DIM] for c in range(CHUNKS_PER_STEP)],
            axis=0))
    mixed = jnp.concatenate(mixed_cols, axis=1)

    h_gg = proj(GG0, _END)

    att = jnp.concatenate([jnp.concatenate(head_out[i], axis=1) for i in range(BLOCKS_PER_STEP)],
                          axis=0)
    y_a = (att * _silu(h_za)).astype(BF16)
    y_g = (_gelu(h_u) * mixed * _silu(h_zg)).astype(BF16)

    br_a = jnp.dot(y_a, w_ba_ref[...], preferred_element_type=F32)
    br_g = jnp.dot(y_g, w_bg_ref[...], preferred_element_type=F32)
    merged = jax.nn.sigmoid(h_ga) * br_a + jax.nn.sigmoid(h_gg) * br_g
    out = jnp.dot(merged.astype(BF16), w_out_ref[...], preferred_element_type=F32) + b_out_ref[...]
    o_ref[...] = _layer_norm(ALPHA * x + out, ln_g_ref[...], ln_b_ref[...])

    xb_next = x_next_ref[...].astype(BF16)
    xb_ref[write_slot] = xb_next
    ahead_ref[write_slot] = _project_ahead(xb_next, w_in_ref, b_in_ref)


def _const_spec(shape, layer):
    zeros = (0,) * len(shape)
    return pl.BlockSpec((None,) + tuple(shape), lambda t: (layer,) + zeros,
                        pipeline_mode=pl.Buffered(1))


def _layer_call(layer, x, params):
    batch, seq, _ = x.shape
    assert seq % TOKENS_PER_STEP == 0
    tiles_per_seq = seq // TOKENS_PER_STEP
    n_tiles = batch * tiles_per_seq
    (sinks, w_in, b_in, gln_g, gln_b, ws, bs, w_ba, w_bg, w_out, b_out, ln_g, ln_b) = params

    def tile_block(tile):
        return (tile // tiles_per_seq, tile % tiles_per_seq, 0)

    block = (None, TOKENS_PER_STEP, D_MODEL)
    x_spec = pl.BlockSpec(block, lambda t: tile_block(t))
    x_next_spec = pl.BlockSpec(block, lambda t: tile_block(jnp.minimum(t + 1, n_tiles - 1)))
    in_specs = [
        pl.BlockSpec(memory_space=pltpu.SMEM),
        x_spec,
        x_next_spec,
        _const_spec((D_MODEL, IN_WIDTH), layer),
        _const_spec((1, IN_WIDTH), layer),
        _const_spec((1, GMLP_WIDTH), layer),
        _const_spec((1, GMLP_WIDTH), layer),
        _const_spec((GMLP_GROUPS, CHUNK, CHUNK), layer),
        _const_spec((GMLP_GROUPS, CHUNK, 1), layer),
        _const_spec((ATTN_WIDTH, D_MODEL), layer),
        _const_spec((GMLP_WIDTH, D_MODEL), layer),
        _const_spec((D_MODEL, D_MODEL), layer),
        _const_spec((1, D_MODEL), layer),
        _const_spec((1, D_MODEL), layer),
        _const_spec((1, D_MODEL), layer),
    ]
    return pl.pallas_call(
        functools.partial(_layer_kernel, tiles_per_seq=tiles_per_seq),
        grid=(n_tiles,),
        in_specs=in_specs,
        out_specs=x_spec,
        out_shape=jax.ShapeDtypeStruct(x.shape, x.dtype),
        scratch_shapes=[
            pltpu.VMEM((2, ATTN_BLOCK, 2 * KV_WIDTH), BF16),
            pltpu.VMEM((2, N_KV_HEADS, GROUP * ATTN_BLOCK, 2 * ATTN_BLOCK), F32),
            pltpu.VMEM((2, TOKENS_PER_STEP, AHEAD_WIDTH), F32),
            pltpu.VMEM((2, TOKENS_PER_STEP, D_MODEL), BF16),
        ],
        compiler_params=pltpu.CompilerParams(
            dimension_semantics=("arbitrary",),
            vmem_limit_bytes=VMEM_LIMIT_BYTES),
        name=f"trunk_layer_{layer}",
    )(sinks[layer], x, x, w_in, b_in, gln_g, gln_b, ws, bs, w_ba, w_bg, w_out, b_out, ln_g, ln_b)


def kernel(x, w_in, b_in, attn_sinks, gmlp_ln_g, gmlp_ln_b, w_spatial, b_spatial,
           w_branch_attn, w_branch_gmlp, w_out, b_out, ln_g, ln_b):
    params = (
        attn_sinks,
        w_in.astype(BF16),
        b_in[:, None, :],
        gmlp_ln_g[:, None, :],
        gmlp_ln_b[:, None, :],
        w_spatial,
        b_spatial[..., None],
        w_branch_attn.astype(BF16),
        w_branch_gmlp.astype(BF16),
        w_out.astype(BF16),
        b_out[:, None, :],
        ln_g[:, None, :],
        ln_b[:, None, :],
    )
    for layer in range(DEPTH):
        x = _layer_call(layer, x, params)
    return x
```

```python
import functools
import math

import jax
import jax.numpy as jnp
from jax import lax
from jax.experimental import pallas as pl
from jax.experimental.pallas import tpu as pltpu

D_MODEL = 1024
DEPTH = 4
N_Q_HEADS = 8
N_KV_HEADS = 2
HEAD_DIM = 64
GROUP = N_Q_HEADS // N_KV_HEADS
ATTN_WIDTH = N_Q_HEADS * HEAD_DIM
KV_WIDTH = N_KV_HEADS * HEAD_DIM
WINDOW = 128
ATTN_BLOCK = 128
GMLP_GROUPS = 4
GMLP_GROUP_DIM = 128
GMLP_WIDTH = GMLP_GROUPS * GMLP_GROUP_DIM
CHUNK = 128
SPLITS = (ATTN_WIDTH, KV_WIDTH, KV_WIDTH, ATTN_WIDTH,
          GMLP_WIDTH, GMLP_WIDTH, GMLP_WIDTH, D_MODEL, D_MODEL)
IN_WIDTH = sum(SPLITS)
ALPHA = (2 * DEPTH) ** 0.25
LN_EPS = 1e-5
LOG2E = math.log2(math.e)

_OFFS = [0]
for _w in SPLITS:
    _OFFS.append(_OFFS[-1] + _w)
(Q0, K0, V0, ZA0, U0, VG0, ZG0, GA0, GG0, _END) = _OFFS
QKV_WIDTH = ZA0 - Q0
MXU_TILE = 256
AHEAD_VG = MXU_TILE
AHEAD_WIDTH = QKV_WIDTH + AHEAD_VG

TOKENS_PER_STEP = 512
BLOCKS_PER_STEP = TOKENS_PER_STEP // ATTN_BLOCK
CHUNKS_PER_STEP = TOKENS_PER_STEP // CHUNK
VMEM_LIMIT_BYTES = 56 * 1024 * 1024

BF16 = jnp.bfloat16
F32 = jnp.float32


def _silu(x):
    return x * jax.nn.sigmoid(x)


def _gelu(x):
    return 0.5 * x * (1.0 + lax.erf(x * math.sqrt(0.5)))


def _build_bias(bias_ref):
    qi = lax.broadcasted_iota(jnp.int32, (ATTN_BLOCK, 2 * ATTN_BLOCK), 0)
    kj = lax.broadcasted_iota(jnp.int32, (ATTN_BLOCK, 2 * ATTN_BLOCK), 1)
    dist = qi + ATTN_BLOCK - kj
    valid = (dist >= 0) & (dist < WINDOW)
    valid_first = valid & (kj >= ATTN_BLOCK)
    distf = dist.astype(F32)
    for h in range(N_Q_HEADS):
        slope = 2.0 ** (-8.0 * (h + 1.0) / N_Q_HEADS)
        hk, g = divmod(h, GROUP)
        rows = slice(g * ATTN_BLOCK, (g + 1) * ATTN_BLOCK)
        b = (-slope * LOG2E) * distf
        bias_ref[0, hk, rows, :] = jnp.where(valid, b, -jnp.inf)
        bias_ref[1, hk, rows, :] = jnp.where(valid_first, b, -jnp.inf)


def _layer_norm(y, g, b):
    mu = jnp.mean(y, axis=-1, keepdims=True)
    yc = y - mu
    var = jnp.mean(yc * yc, axis=-1, keepdims=True)
    return yc * lax.rsqrt(var + LN_EPS) * g + b


def _project_ahead(xb, w_in_ref, b_in_ref):
    h_qkv = jnp.dot(xb, w_in_ref[:, Q0:ZA0], preferred_element_type=F32) + b_in_ref[:, Q0:ZA0]
    h_vg = (jnp.dot(xb, w_in_ref[:, VG0:VG0 + AHEAD_VG], preferred_element_type=F32)
            + b_in_ref[:, VG0:VG0 + AHEAD_VG])
    return jnp.concatenate([h_qkv, h_vg], axis=1)


def _layer_kernel(sinks_ref, x_ref, x_next_ref, w_in_ref, b_in_ref, gln_g_ref, gln_b_ref,
                  ws_ref, bs_ref, w_ba_ref, w_bg_ref, w_out_ref, b_out_ref, ln_g_ref, ln_b_ref,
                  o_ref, kv_ref, bias_ref, ahead_ref, xb_ref, *, tiles_per_seq):
    t = pl.program_id(0)
    j = t % tiles_per_seq
    read_slot = t % 2
    write_slot = 1 - read_slot

    @pl.when(t == 0)
    def _():
        _build_bias(bias_ref)
        xb0 = x_ref[...].astype(BF16)
        xb_ref[0] = xb0
        ahead_ref[0] = _project_ahead(xb0, w_in_ref, b_in_ref)

    @pl.when(j == 0)
    def _():
        kv_ref[read_slot] = jnp.zeros(kv_ref.shape[1:], kv_ref.dtype)

    x = x_ref[...]

    def proj(lo, hi):
        return (jnp.dot(xb_ref[read_slot], w_in_ref[:, lo:hi], preferred_element_type=F32)
                + b_in_ref[:, lo:hi])

    ahead = ahead_ref[read_slot]
    q = (ahead[:, :ATTN_WIDTH] * (HEAD_DIM ** -0.5 * LOG2E)).astype(BF16)
    kv = ahead[:, ATTN_WIDTH:QKV_WIDTH].astype(BF16)
    kv_ext = jnp.concatenate([kv_ref[read_slot], kv], axis=0)
    kv_ref[write_slot] = kv[TOKENS_PER_STEP - ATTN_BLOCK:, :]

    first_sel = jnp.where(j == 0, 1, 0)
    k_heads = [kv_ext[:, hk * HEAD_DIM:(hk + 1) * HEAD_DIM] for hk in range(N_KV_HEADS)]
    v_heads = [kv_ext[:, KV_WIDTH + hk * HEAD_DIM:KV_WIDTH + (hk + 1) * HEAD_DIM]
               for hk in range(N_KV_HEADS)]
    row_group = lax.broadcasted_iota(jnp.int32, (GROUP * ATTN_BLOCK, 1), 0) // ATTN_BLOCK
    sink_cols = []
    for hk in range(N_KV_HEADS):
        sink_col = jnp.zeros((GROUP * ATTN_BLOCK, 1), F32)
        for g in range(GROUP):
            sink_col = jnp.where(row_group == g, sinks_ref[hk * GROUP + g] * LOG2E, sink_col)
        sink_cols.append(sink_col)

    def attn_probs(hk, i):
        r0 = i * ATTN_BLOCK
        qs = jnp.concatenate(
            [q[r0:r0 + ATTN_BLOCK, (hk * GROUP + g) * HEAD_DIM:(hk * GROUP + g + 1) * HEAD_DIM]
             for g in range(GROUP)], axis=0)
        kk = k_heads[hk][r0:r0 + 2 * ATTN_BLOCK, :]
        s = lax.dot_general(qs, kk, (((1,), (1,)), ((), ())), preferred_element_type=F32)
        s = s + (bias_ref[first_sel, hk] if i == 0 else bias_ref[0, hk])
        m = jnp.maximum(jnp.max(s, axis=-1, keepdims=True), sink_cols[hk])
        p = jnp.exp2(s - m)
        denom = jnp.sum(p, axis=-1, keepdims=True) + jnp.exp2(sink_cols[hk] - m)
        return p.astype(BF16), denom

    head_out = [[None] * N_Q_HEADS for _ in range(BLOCKS_PER_STEP)]

    def attn_values(hk, i, p, denom):
        r0 = i * ATTN_BLOCK
        vv = v_heads[hk][r0:r0 + 2 * ATTN_BLOCK, :]
        o = jnp.dot(p, vv, preferred_element_type=F32) / denom
        for g in range(GROUP):
            head_out[i][hk * GROUP + g] = o[g * ATTN_BLOCK:(g + 1) * ATTN_BLOCK, :]

    tiles = {}

    def proj_slice(lo):
        tiles[lo] = proj(lo, lo + MXU_TILE)

    def gather(lo0, hi0):
        return jnp.concatenate([tiles[lo] for lo in range(lo0, hi0, MXU_TILE)], axis=1)

    def spatial_mixing():
        h_vg = jnp.concatenate([ahead[:, QKV_WIDTH:], gather(VG0 + AHEAD_VG, ZG0)], axis=1)
        vn = _layer_norm(_gelu(h_vg), gln_g_ref[...], gln_b_ref[...]).astype(BF16)
        causal = (lax.broadcasted_iota(jnp.int32, (CHUNK, CHUNK), 0)
                  >= lax.broadcasted_iota(jnp.int32, (CHUNK, CHUNK), 1))
        mixed_cols = []
        for grp in range(GMLP_GROUPS):
            c0 = grp * GMLP_GROUP_DIM
            w = jnp.where(causal, ws_ref[grp], 0.0).astype(BF16)
            rhs = jnp.concatenate([vn[c * CHUNK:(c + 1) * CHUNK, c0:c0 + GMLP_GROUP_DIM]
                                   for c in range(CHUNKS_PER_STEP)], axis=1)
            mix = jnp.dot(w, rhs, preferred_element_type=F32) + bs_ref[grp]
            mixed_cols.append(jnp.concatenate(
                [mix[:, c * GMLP_GROUP_DIM:(c + 1) * GMLP_GROUP_DIM]
                 for c in range(CHUNKS_PER_STEP)], axis=0))
        tiles["mixed"] = jnp.concatenate(mixed_cols, axis=1)

    order = [VG0 + AHEAD_VG, U0, U0 + MXU_TILE, ZG0, None, ZG0 + MXU_TILE, ZA0, ZA0 + MXU_TILE]
    order += list(range(GA0, GG0, MXU_TILE))
    fillers = [spatial_mixing if lo is None else functools.partial(proj_slice, lo) for lo in order]
    fillers.reverse()

    in_flight = []
    for unit in [(hk, i) for i in range(BLOCKS_PER_STEP) for hk in range(N_KV_HEADS)]:
        in_flight.append(unit + attn_probs(*unit))
        fillers.pop()()
        if len(in_flight) > 2:
            attn_values(*in_flight.pop(0))
    for unit in in_flight:
        fillers.pop()()
        attn_values(*unit)
    while fillers:
        fillers.pop()()

    h_u, h_zg, h_za = gather(U0, VG0), gather(ZG0, GA0), gather(ZA0, U0)
    h_ga = gather(GA0, GG0)
    mixed = tiles["mixed"]

    h_gg = proj(GG0, _END)

    att = jnp.concatenate([jnp.concatenate(head_out[i], axis=1) for i in range(BLOCKS_PER_STEP)],
                          axis=0)
    y_a = (att * _silu(h_za)).astype(BF16)
    y_g = (_gelu(h_u) * mixed * _silu(h_zg)).astype(BF16)

    br_a = jnp.dot(y_a, w_ba_ref[...], preferred_element_type=F32)
    br_g = jnp.dot(y_g, w_bg_ref[...], preferred_element_type=F32)
    merged = jax.nn.sigmoid(h_ga) * br_a + jax.nn.sigmoid(h_gg) * br_g
    out = jnp.dot(merged.astype(BF16), w_out_ref[...], preferred_element_type=F32) + b_out_ref[...]
    o_ref[...] = _layer_norm(ALPHA * x + out, ln_g_ref[...], ln_b_ref[...])

    xb_next = x_next_ref[...].astype(BF16)
    xb_ref[write_slot] = xb_next
    ahead_ref[write_slot] = _project_ahead(xb_next, w_in_ref, b_in_ref)


def _const_spec(shape, layer):
    zeros = (0,) * len(shape)
    return pl.BlockSpec((None,) + tuple(shape), lambda t: (layer,) + zeros,
                        pipeline_mode=pl.Buffered(1))


def _layer_call(layer, x, params):
    batch, seq, _ = x.shape
    assert seq % TOKENS_PER_STEP == 0
    tiles_per_seq = seq // TOKENS_PER_STEP
    n_tiles = batch * tiles_per_seq
    (sinks, w_in, b_in, gln_g, gln_b, ws, bs, w_ba, w_bg, w_out, b_out, ln_g, ln_b) = params

    def tile_block(tile):
        return (tile // tiles_per_seq, tile % tiles_per_seq, 0)

    block = (None, TOKENS_PER_STEP, D_MODEL)
    x_spec = pl.BlockSpec(block, lambda t: tile_block(t))
    x_next_spec = pl.BlockSpec(block, lambda t: tile_block(jnp.minimum(t + 1, n_tiles - 1)))
    in_specs = [
        pl.BlockSpec(memory_space=pltpu.SMEM),
        x_spec,
        x_next_spec,
        _const_spec((D_MODEL, IN_WIDTH), layer),
        _const_spec((1, IN_WIDTH), layer),
        _const_spec((1, GMLP_WIDTH), layer),
        _const_spec((1, GMLP_WIDTH), layer),
        _const_spec((GMLP_GROUPS, CHUNK, CHUNK), layer),
        _const_spec((GMLP_GROUPS, CHUNK, 1), layer),
        _const_spec((ATTN_WIDTH, D_MODEL), layer),
        _const_spec((GMLP_WIDTH, D_MODEL), layer),
        _const_spec((D_MODEL, D_MODEL), layer),
        _const_spec((1, D_MODEL), layer),
        _const_spec((1, D_MODEL), layer),
        _const_spec((1, D_MODEL), layer),
    ]
    return pl.pallas_call(
        functools.partial(_layer_kernel, tiles_per_seq=tiles_per_seq),
        grid=(n_tiles,),
        in_specs=in_specs,
        out_specs=x_spec,
        out_shape=jax.ShapeDtypeStruct(x.shape, x.dtype),
        scratch_shapes=[
            pltpu.VMEM((2, ATTN_BLOCK, 2 * KV_WIDTH), BF16),
            pltpu.VMEM((2, N_KV_HEADS, GROUP * ATTN_BLOCK, 2 * ATTN_BLOCK), F32),
            pltpu.VMEM((2, TOKENS_PER_STEP, AHEAD_WIDTH), F32),
            pltpu.VMEM((2, TOKENS_PER_STEP, D_MODEL), BF16),
        ],
        compiler_params=pltpu.CompilerParams(
            dimension_semantics=("arbitrary",),
            vmem_limit_bytes=VMEM_LIMIT_BYTES),
        name=f"trunk_layer_{layer}",
    )(sinks[layer], x, x, w_in, b_in, gln_g, gln_b, ws, bs, w_ba, w_bg, w_out, b_out, ln_g, ln_b)


def kernel(x, w_in, b_in, attn_sinks, gmlp_ln_g, gmlp_ln_b, w_spatial, b_spatial,
           w_branch_attn, w_branch_gmlp, w_out, b_out, ln_g, ln_b):
    params = (
        attn_sinks,
        w_in.astype(BF16),
        b_in[:, None, :],
        gmlp_ln_g[:, None, :],
        gmlp_ln_b[:, None, :],
        w_spatial,
        b_spatial[..., None],
        w_branch_attn.astype(BF16),
        w_branch_gmlp.astype(BF16),
        w_out.astype(BF16),
        b_out[:, None, :],
        ln_g[:, None, :],
        ln_b[:, None, :],
    )
    for layer in range(DEPTH):
        x = _layer_call(layer, x, params)
    return x
```

```python
import functools
import math

import jax
import jax.numpy as jnp
from jax import lax
from jax.experimental import pallas as pl
from jax.experimental.pallas import tpu as pltpu

D_MODEL = 1024
DEPTH = 4
N_Q_HEADS = 8
N_KV_HEADS = 2
HEAD_DIM = 64
GROUP = N_Q_HEADS // N_KV_HEADS
ATTN_WIDTH = N_Q_HEADS * HEAD_DIM
KV_WIDTH = N_KV_HEADS * HEAD_DIM
WINDOW = 128
ATTN_BLOCK = 128
GMLP_GROUPS = 4
GMLP_GROUP_DIM = 128
GMLP_WIDTH = GMLP_GROUPS * GMLP_GROUP_DIM
CHUNK = 128
SPLITS = (ATTN_WIDTH, KV_WIDTH, KV_WIDTH, ATTN_WIDTH,
          GMLP_WIDTH, GMLP_WIDTH, GMLP_WIDTH, D_MODEL, D_MODEL)
IN_WIDTH = sum(SPLITS)
ALPHA = (2 * DEPTH) ** 0.25
LN_EPS = 1e-5
LOG2E = math.log2(math.e)

_OFFS = [0]
for _w in SPLITS:
    _OFFS.append(_OFFS[-1] + _w)
(Q0, K0, V0, ZA0, U0, VG0, ZG0, GA0, GG0, _END) = _OFFS
QKV_WIDTH = ZA0 - Q0
MXU_TILE = 256
AHEAD_VG = MXU_TILE
AHEAD_WIDTH = QKV_WIDTH + AHEAD_VG

TOKENS_PER_STEP = 512
BLOCKS_PER_STEP = TOKENS_PER_STEP // ATTN_BLOCK
CHUNKS_PER_STEP = TOKENS_PER_STEP // CHUNK
VMEM_LIMIT_BYTES = 56 * 1024 * 1024
WEIGHT_CHUNK_ROWS = 64

BF16 = jnp.bfloat16
F32 = jnp.float32


def _silu(x):
    return x * jax.nn.sigmoid(x)


def _gelu(x):
    return 0.5 * x * (1.0 + lax.erf(x * math.sqrt(0.5)))


def _build_bias(bias_ref):
    qi = lax.broadcasted_iota(jnp.int32, (ATTN_BLOCK, 2 * ATTN_BLOCK), 0)
    kj = lax.broadcasted_iota(jnp.int32, (ATTN_BLOCK, 2 * ATTN_BLOCK), 1)
    dist = qi + ATTN_BLOCK - kj
    valid = (dist >= 0) & (dist < WINDOW)
    valid_first = valid & (kj >= ATTN_BLOCK)
    distf = dist.astype(F32)
    for h in range(N_Q_HEADS):
        slope = 2.0 ** (-8.0 * (h + 1.0) / N_Q_HEADS)
        hk, g = divmod(h, GROUP)
        rows = slice(g * ATTN_BLOCK, (g + 1) * ATTN_BLOCK)
        b = (-slope * LOG2E) * distf
        bias_ref[0, hk, rows, :] = jnp.where(valid, b, -jnp.inf)
        bias_ref[1, hk, rows, :] = jnp.where(valid_first, b, -jnp.inf)


def _layer_norm(y, g, b):
    mu = jnp.mean(y, axis=-1, keepdims=True)
    yc = y - mu
    var = jnp.mean(yc * yc, axis=-1, keepdims=True)
    return yc * lax.rsqrt(var + LN_EPS) * g + b


def _project_ahead(xb, w_in_ref, b_in_ref):
    h_qkv = jnp.dot(xb, w_in_ref[:, Q0:ZA0], preferred_element_type=F32) + b_in_ref[:, Q0:ZA0]
    h_vg = (jnp.dot(xb, w_in_ref[:, VG0:VG0 + AHEAD_VG], preferred_element_type=F32)
            + b_in_ref[:, VG0:VG0 + AHEAD_VG])
    return jnp.concatenate([h_qkv, h_vg], axis=1)


def _load_weight_as_bf16(src_hbm, layer, dst_ref, stage_ref, sem_ref):
    rows, cols = dst_ref.shape
    n_chunks = rows // WEIGHT_CHUNK_ROWS

    def chunk_copy(c, slot):
        return pltpu.make_async_copy(
            src_hbm.at[layer, pl.ds(c * WEIGHT_CHUNK_ROWS, WEIGHT_CHUNK_ROWS), :],
            stage_ref.at[slot, :, pl.ds(0, cols)],
            sem_ref.at[slot])

    chunk_copy(0, 0).start()

    def body(c, carry):
        slot = c % 2

        @pl.when(c + 1 < n_chunks)
        def _():
            chunk_copy(c + 1, 1 - slot).start()

        chunk_copy(c, slot).wait()
        r0 = pl.multiple_of(c * WEIGHT_CHUNK_ROWS, WEIGHT_CHUNK_ROWS)
        dst_ref[pl.ds(r0, WEIGHT_CHUNK_ROWS), :] = stage_ref[slot, :, :cols].astype(BF16)
        return carry

    lax.fori_loop(0, n_chunks, body, 0)


def _layer_kernel(sinks_ref, x_ref, x_next_ref, w_in_hbm, b_in_ref, gln_g_ref, gln_b_ref,
                  ws_ref, bs_ref, w_ba_hbm, w_bg_hbm, w_out_hbm, b_out_ref, ln_g_ref, ln_b_ref,
                  o_ref, kv_ref, bias_ref, ahead_ref, xb_ref,
                  w_in_ref, w_ba_ref, w_bg_ref, w_out_ref, stage_ref, stage_sem,
                  *, layer, tiles_per_seq):
    t = pl.program_id(0)
    j = t % tiles_per_seq
    read_slot = t % 2
    write_slot = 1 - read_slot

    @pl.when(t == 0)
    def _():
        for src, dst in ((w_in_hbm, w_in_ref), (w_ba_hbm, w_ba_ref), (w_bg_hbm, w_bg_ref),
                         (w_out_hbm, w_out_ref)):
            _load_weight_as_bf16(src, layer, dst, stage_ref, stage_sem)
        _build_bias(bias_ref)
        xb0 = x_ref[...].astype(BF16)
        xb_ref[0] = xb0
        ahead_ref[0] = _project_ahead(xb0, w_in_ref, b_in_ref)

    @pl.when(j == 0)
    def _():
        kv_ref[read_slot] = jnp.zeros(kv_ref.shape[1:], kv_ref.dtype)

    x = x_ref[...]

    def proj(lo, hi):
        return (jnp.dot(xb_ref[read_slot], w_in_ref[:, lo:hi], preferred_element_type=F32)
                + b_in_ref[:, lo:hi])

    ahead = ahead_ref[read_slot]
    q = (ahead[:, :ATTN_WIDTH] * (HEAD_DIM ** -0.5 * LOG2E)).astype(BF16)
    kv = ahead[:, ATTN_WIDTH:QKV_WIDTH].astype(BF16)
    kv_ext = jnp.concatenate([kv_ref[read_slot], kv], axis=0)
    kv_ref[write_slot] = kv[TOKENS_PER_STEP - ATTN_BLOCK:, :]

    first_sel = jnp.where(j == 0, 1, 0)
    k_heads = [kv_ext[:, hk * HEAD_DIM:(hk + 1) * HEAD_DIM] for hk in range(N_KV_HEADS)]
    v_heads = [kv_ext[:, KV_WIDTH + hk * HEAD_DIM:KV_WIDTH + (hk + 1) * HEAD_DIM]
               for hk in range(N_KV_HEADS)]
    row_group = lax.broadcasted_iota(jnp.int32, (GROUP * ATTN_BLOCK, 1), 0) // ATTN_BLOCK
    sink_cols = []
    for hk in range(N_KV_HEADS):
        sink_col = jnp.zeros((GROUP * ATTN_BLOCK, 1), F32)
        for g in range(GROUP):
            sink_col = jnp.where(row_group == g, sinks_ref[hk * GROUP + g] * LOG2E, sink_col)
        sink_cols.append(sink_col)

    def attn_probs(hk, i):
        r0 = i * ATTN_BLOCK
        qs = jnp.concatenate(
            [q[r0:r0 + ATTN_BLOCK, (hk * GROUP + g) * HEAD_DIM:(hk * GROUP + g + 1) * HEAD_DIM]
             for g in range(GROUP)], axis=0)
        kk = k_heads[hk][r0:r0 + 2 * ATTN_BLOCK, :]
        s = lax.dot_general(qs, kk, (((1,), (1,)), ((), ())), preferred_element_type=F32)
        s = s + (bias_ref[first_sel, hk] if i == 0 else bias_ref[0, hk])
        m = jnp.maximum(jnp.max(s, axis=-1, keepdims=True), sink_cols[hk])
        p = jnp.exp2(s - m)
        denom = jnp.sum(p, axis=-1, keepdims=True) + jnp.exp2(sink_cols[hk] - m)
        return p.astype(BF16), denom

    head_out = [[None] * N_Q_HEADS for _ in range(BLOCKS_PER_STEP)]

    def attn_values(hk, i, p, denom):
        r0 = i * ATTN_BLOCK
        vv = v_heads[hk][r0:r0 + 2 * ATTN_BLOCK, :]
        o = jnp.dot(p, vv, preferred_element_type=F32) / denom
        for g in range(GROUP):
            head_out[i][hk * GROUP + g] = o[g * ATTN_BLOCK:(g + 1) * ATTN_BLOCK, :]

    tiles = {}

    def proj_slice(lo):
        tiles[lo] = proj(lo, lo + MXU_TILE)

    def gather(lo0, hi0):
        return jnp.concatenate([tiles[lo] for lo in range(lo0, hi0, MXU_TILE)], axis=1)

    def spatial_mixing():
        h_vg = jnp.concatenate([ahead[:, QKV_WIDTH:], gather(VG0 + AHEAD_VG, ZG0)], axis=1)
        vn = _layer_norm(_gelu(h_vg), gln_g_ref[...], gln_b_ref[...]).astype(BF16)
        causal = (lax.broadcasted_iota(jnp.int32, (CHUNK, CHUNK), 0)
                  >= lax.broadcasted_iota(jnp.int32, (CHUNK, CHUNK), 1))
        mixed_cols = []
        for grp in range(GMLP_GROUPS):
            c0 = grp * GMLP_GROUP_DIM
            w = jnp.where(causal, ws_ref[grp], 0.0).astype(BF16)
            rhs = jnp.concatenate([vn[c * CHUNK:(c + 1) * CHUNK, c0:c0 + GMLP_GROUP_DIM]
                                   for c in range(CHUNKS_PER_STEP)], axis=1)
            mix = jnp.dot(w, rhs, preferred_element_type=F32) + bs_ref[grp]
            mixed_cols.append(jnp.concatenate(
                [mix[:, c * GMLP_GROUP_DIM:(c + 1) * GMLP_GROUP_DIM]
                 for c in range(CHUNKS_PER_STEP)], axis=0))
        tiles["mixed"] = jnp.concatenate(mixed_cols, axis=1)

    order = [VG0 + AHEAD_VG, U0, U0 + MXU_TILE, ZG0, None, ZG0 + MXU_TILE, ZA0, ZA0 + MXU_TILE]
    order += list(range(GA0, GG0, MXU_TILE))
    fillers = [spatial_mixing if lo is None else functools.partial(proj_slice, lo) for lo in order]
    fillers.reverse()

    in_flight = []
    for unit in [(hk, i) for i in range(BLOCKS_PER_STEP) for hk in range(N_KV_HEADS)]:
        in_flight.append(unit + attn_probs(*unit))
        fillers.pop()()
        if len(in_flight) > 2:
            attn_values(*in_flight.pop(0))
    for unit in in_flight:
        fillers.pop()()
        attn_values(*unit)
    while fillers:
        fillers.pop()()

    h_u, h_zg, h_za = gather(U0, VG0), gather(ZG0, GA0), gather(ZA0, U0)
    h_ga = gather(GA0, GG0)
    mixed = tiles["mixed"]

    h_gg = proj(GG0, _END)

    att = jnp.concatenate([jnp.concatenate(head_out[i], axis=1) for i in range(BLOCKS_PER_STEP)],
                          axis=0)
    y_a = (att * _silu(h_za)).astype(BF16)
    y_g = (_gelu(h_u) * mixed * _silu(h_zg)).astype(BF16)

    br_a = jnp.dot(y_a, w_ba_ref[...], preferred_element_type=F32)
    br_g = jnp.dot(y_g, w_bg_ref[...], preferred_element_type=F32)
    merged = jax.nn.sigmoid(h_ga) * br_a + jax.nn.sigmoid(h_gg) * br_g
    out = jnp.dot(merged.astype(BF16), w_out_ref[...], preferred_element_type=F32) + b_out_ref[...]
    o_ref[...] = _layer_norm(ALPHA * x + out, ln_g_ref[...], ln_b_ref[...])

    xb_next = x_next_ref[...].astype(BF16)
    xb_ref[write_slot] = xb_next
    ahead_ref[write_slot] = _project_ahead(xb_next, w_in_ref, b_in_ref)


def _const_spec(shape, layer):
    zeros = (0,) * len(shape)
    return pl.BlockSpec((None,) + tuple(shape), lambda t: (layer,) + zeros,
                        pipeline_mode=pl.Buffered(1))


def _layer_call(layer, x, params):
    batch, seq, _ = x.shape
    assert seq % TOKENS_PER_STEP == 0
    tiles_per_seq = seq // TOKENS_PER_STEP
    n_tiles = batch * tiles_per_seq
    (sinks, w_in, b_in, gln_g, gln_b, ws, bs, w_ba, w_bg, w_out, b_out, ln_g, ln_b) = params

    def tile_block(tile):
        return (tile // tiles_per_seq, tile % tiles_per_seq, 0)

    block = (None, TOKENS_PER_STEP, D_MODEL)
    x_spec = pl.BlockSpec(block, lambda t: tile_block(t))
    x_next_spec = pl.BlockSpec(block, lambda t: tile_block(jnp.minimum(t + 1, n_tiles - 1)))
    in_specs = [
        pl.BlockSpec(memory_space=pltpu.SMEM),
        x_spec,
        x_next_spec,
        pl.BlockSpec(memory_space=pl.ANY),
        _const_spec((1, IN_WIDTH), layer),
        _const_spec((1, GMLP_WIDTH), layer),
        _const_spec((1, GMLP_WIDTH), layer),
        _const_spec((GMLP_GROUPS, CHUNK, CHUNK), layer),
        _const_spec((GMLP_GROUPS, CHUNK, 1), layer),
        pl.BlockSpec(memory_space=pl.ANY),
        pl.BlockSpec(memory_space=pl.ANY),
        pl.BlockSpec(memory_space=pl.ANY),
        _const_spec((1, D_MODEL), layer),
        _const_spec((1, D_MODEL), layer),
        _const_spec((1, D_MODEL), layer),
    ]
    return pl.pallas_call(
        functools.partial(_layer_kernel, layer=layer, tiles_per_seq=tiles_per_seq),
        grid=(n_tiles,),
        in_specs=in_specs,
        out_specs=x_spec,
        out_shape=jax.ShapeDtypeStruct(x.shape, x.dtype),
        scratch_shapes=[
            pltpu.VMEM((2, ATTN_BLOCK, 2 * KV_WIDTH), BF16),
            pltpu.VMEM((2, N_KV_HEADS, GROUP * ATTN_BLOCK, 2 * ATTN_BLOCK), F32),
            pltpu.VMEM((2, TOKENS_PER_STEP, AHEAD_WIDTH), F32),
            pltpu.VMEM((2, TOKENS_PER_STEP, D_MODEL), BF16),
            pltpu.VMEM((D_MODEL, IN_WIDTH), BF16),
            pltpu.VMEM((ATTN_WIDTH, D_MODEL), BF16),
            pltpu.VMEM((GMLP_WIDTH, D_MODEL), BF16),
            pltpu.VMEM((D_MODEL, D_MODEL), BF16),
            pltpu.VMEM((2, WEIGHT_CHUNK_ROWS, IN_WIDTH), F32),
            pltpu.SemaphoreType.DMA((2,)),
        ],
        compiler_params=pltpu.CompilerParams(
            dimension_semantics=("arbitrary",),
            vmem_limit_bytes=VMEM_LIMIT_BYTES),
        name=f"trunk_layer_{layer}",
    )(sinks[layer], x, x, w_in, b_in, gln_g, gln_b, ws, bs, w_ba, w_bg, w_out, b_out, ln_g, ln_b)


def kernel(x, w_in, b_in, attn_sinks, gmlp_ln_g, gmlp_ln_b, w_spatial, b_spatial,
           w_branch_attn, w_branch_gmlp, w_out, b_out, ln_g, ln_b):
    params = (
        attn_sinks,
        w_in,
        b_in[:, None, :],
        gmlp_ln_g[:, None, :],
        gmlp_ln_b[:, None, :],
        w_spatial,
        b_spatial[..., None],
        w_branch_attn,
        w_branch_gmlp,
        w_out,
        b_out[:, None, :],
        ln_g[:, None, :],
        ln_b[:, None, :],
    )
    for layer in range(DEPTH):
        x = _layer_call(layer, x, params)
    return x
```

```python
import functools
import math

import jax
import jax.numpy as jnp
from jax import lax
from jax.experimental import pallas as pl
from jax.experimental.pallas import tpu as pltpu

D_MODEL = 1024
DEPTH = 4
N_Q_HEADS = 8
N_KV_HEADS = 2
HEAD_DIM = 64
GROUP = N_Q_HEADS // N_KV_HEADS
ATTN_WIDTH = N_Q_HEADS * HEAD_DIM
KV_WIDTH = N_KV_HEADS * HEAD_DIM
WINDOW = 128
ATTN_BLOCK = 128
GMLP_GROUPS = 4
GMLP_GROUP_DIM = 128
GMLP_WIDTH = GMLP_GROUPS * GMLP_GROUP_DIM
CHUNK = 128
SPLITS = (ATTN_WIDTH, KV_WIDTH, KV_WIDTH, ATTN_WIDTH,
          GMLP_WIDTH, GMLP_WIDTH, GMLP_WIDTH, D_MODEL, D_MODEL)
IN_WIDTH = sum(SPLITS)
ALPHA = (2 * DEPTH) ** 0.25
LN_EPS = 1e-5
LOG2E = math.log2(math.e)

_OFFS = [0]
for _w in SPLITS:
    _OFFS.append(_OFFS[-1] + _w)
(Q0, K0, V0, ZA0, U0, VG0, ZG0, GA0, GG0, _END) = _OFFS
QKV_WIDTH = ZA0 - Q0
MXU_TILE = 256
AHEAD_VG = MXU_TILE
AHEAD_WIDTH = QKV_WIDTH + AHEAD_VG

TOKENS_PER_STEP = 512
BLOCKS_PER_STEP = TOKENS_PER_STEP // ATTN_BLOCK
CHUNKS_PER_STEP = TOKENS_PER_STEP // CHUNK
VMEM_LIMIT_BYTES = 56 * 1024 * 1024
WEIGHT_STAGE_SLOTS = 3
WIDE_CHUNK_ROWS = 64
NARROW_CHUNK_ROWS = 256

BF16 = jnp.bfloat16
F32 = jnp.float32


def _silu(x):
    return x * jax.nn.sigmoid(x)


def _gelu(x):
    return 0.5 * x * (1.0 + lax.erf(x * math.sqrt(0.5)))


def _build_bias(bias_ref):
    qi = lax.broadcasted_iota(jnp.int32, (ATTN_BLOCK, 2 * ATTN_BLOCK), 0)
    kj = lax.broadcasted_iota(jnp.int32, (ATTN_BLOCK, 2 * ATTN_BLOCK), 1)
    dist = qi + ATTN_BLOCK - kj
    valid = (dist >= 0) & (dist < WINDOW)
    valid_first = valid & (kj >= ATTN_BLOCK)
    distf = dist.astype(F32)
    for h in range(N_Q_HEADS):
        slope = 2.0 ** (-8.0 * (h + 1.0) / N_Q_HEADS)
        hk, g = divmod(h, GROUP)
        rows = slice(g * ATTN_BLOCK, (g + 1) * ATTN_BLOCK)
        b = (-slope * LOG2E) * distf
        bias_ref[0, hk, rows, :] = jnp.where(valid, b, -jnp.inf)
        bias_ref[1, hk, rows, :] = jnp.where(valid_first, b, -jnp.inf)


def _layer_norm(y, g, b):
    mu = jnp.mean(y, axis=-1, keepdims=True)
    yc = y - mu
    var = jnp.mean(yc * yc, axis=-1, keepdims=True)
    return yc * lax.rsqrt(var + LN_EPS) * g + b


def _project_ahead(xb, w_in_ref, b_in_ref):
    h_qkv = jnp.dot(xb, w_in_ref[:, Q0:ZA0], preferred_element_type=F32) + b_in_ref[:, Q0:ZA0]
    h_vg = (jnp.dot(xb, w_in_ref[:, VG0:VG0 + AHEAD_VG], preferred_element_type=F32)
            + b_in_ref[:, VG0:VG0 + AHEAD_VG])
    return jnp.concatenate([h_qkv, h_vg], axis=1)


def _load_weight_as_bf16(src_hbm, layer, dst_ref, stage_ref, sem_ref):
    rows, cols = dst_ref.shape
    n_slots, chunk_rows, stage_cols = stage_ref.shape
    assert cols == stage_cols and rows % chunk_rows == 0
    n_chunks = rows // chunk_rows
    ahead = n_slots - 1

    def chunk_copy(c):
        slot = c % n_slots
        return pltpu.make_async_copy(
            src_hbm.at[layer, pl.ds(c * chunk_rows, chunk_rows), :],
            stage_ref.at[slot], sem_ref.at[slot])

    for c in range(min(ahead, n_chunks)):
        chunk_copy(c).start()

    def body(c, carry):
        @pl.when(c + ahead < n_chunks)
        def _():
            chunk_copy(c + ahead).start()

        chunk_copy(c).wait()
        r0 = pl.multiple_of(c * chunk_rows, chunk_rows)
        dst_ref[pl.ds(r0, chunk_rows), :] = stage_ref[c % n_slots].astype(BF16)
        return carry

    lax.fori_loop(0, n_chunks, body, 0)


def _layer_kernel(sinks_ref, x_ref, x_next_ref, w_in_hbm, b_in_ref, gln_g_ref, gln_b_ref,
                  ws_ref, bs_ref, w_ba_hbm, w_bg_hbm, w_out_hbm, b_out_ref, ln_g_ref, ln_b_ref,
                  o_ref, kv_ref, bias_ref, ahead_ref, xb_ref,
                  w_in_ref, w_ba_ref, w_bg_ref, w_out_ref,
                  wide_stage_ref, wide_sem, narrow_stage_ref, narrow_sem,
                  *, layer, tiles_per_seq):
    t = pl.program_id(0)
    j = t % tiles_per_seq
    read_slot = t % 2
    write_slot = 1 - read_slot

    @pl.when(t == 0)
    def _():
        _load_weight_as_bf16(w_in_hbm, layer, w_in_ref, wide_stage_ref, wide_sem)
        for src, dst in ((w_ba_hbm, w_ba_ref), (w_bg_hbm, w_bg_ref), (w_out_hbm, w_out_ref)):
            _load_weight_as_bf16(src, layer, dst, narrow_stage_ref, narrow_sem)
        _build_bias(bias_ref)
        xb0 = x_ref[...].astype(BF16)
        xb_ref[0] = xb0
        ahead_ref[0] = _project_ahead(xb0, w_in_ref, b_in_ref)

    @pl.when(j == 0)
    def _():
        kv_ref[read_slot] = jnp.zeros(kv_ref.shape[1:], kv_ref.dtype)

    x = x_ref[...]

    def proj(lo, hi):
        return (jnp.dot(xb_ref[read_slot], w_in_ref[:, lo:hi], preferred_element_type=F32)
                + b_in_ref[:, lo:hi])

    ahead = ahead_ref[read_slot]
    q = (ahead[:, :ATTN_WIDTH] * (HEAD_DIM ** -0.5 * LOG2E)).astype(BF16)
    kv = ahead[:, ATTN_WIDTH:QKV_WIDTH].astype(BF16)
    kv_ext = jnp.concatenate([kv_ref[read_slot], kv], axis=0)
    kv_ref[write_slot] = kv[TOKENS_PER_STEP - ATTN_BLOCK:, :]

    first_sel = jnp.where(j == 0, 1, 0)
    k_heads = [kv_ext[:, hk * HEAD_DIM:(hk + 1) * HEAD_DIM] for hk in range(N_KV_HEADS)]
    v_heads = [kv_ext[:, KV_WIDTH + hk * HEAD_DIM:KV_WIDTH + (hk + 1) * HEAD_DIM]
               for hk in range(N_KV_HEADS)]
    row_group = lax.broadcasted_iota(jnp.int32, (GROUP * ATTN_BLOCK, 1), 0) // ATTN_BLOCK
    sink_cols = []
    for hk in range(N_KV_HEADS):
        sink_col = jnp.zeros((GROUP * ATTN_BLOCK, 1), F32)
        for g in range(GROUP):
            sink_col = jnp.where(row_group == g, sinks_ref[hk * GROUP + g] * LOG2E, sink_col)
        sink_cols.append(sink_col)

    def attn_probs(hk, i):
        r0 = i * ATTN_BLOCK
        qs = jnp.concatenate(
            [q[r0:r0 + ATTN_BLOCK, (hk * GROUP + g) * HEAD_DIM:(hk * GROUP + g + 1) * HEAD_DIM]
             for g in range(GROUP)], axis=0)
        kk = k_heads[hk][r0:r0 + 2 * ATTN_BLOCK, :]
        s = lax.dot_general(qs, kk, (((1,), (1,)), ((), ())), preferred_element_type=F32)
        s = s + (bias_ref[first_sel, hk] if i == 0 else bias_ref[0, hk])
        m = jnp.maximum(jnp.max(s, axis=-1, keepdims=True), sink_cols[hk])
        p = jnp.exp2(s - m)
        denom = jnp.sum(p, axis=-1, keepdims=True) + jnp.exp2(sink_cols[hk] - m)
        return p.astype(BF16), denom

    head_out = [[None] * N_Q_HEADS for _ in range(BLOCKS_PER_STEP)]

    def attn_values(hk, i, p, denom):
        r0 = i * ATTN_BLOCK
        vv = v_heads[hk][r0:r0 + 2 * ATTN_BLOCK, :]
        o = jnp.dot(p, vv, preferred_element_type=F32) / denom
        for g in range(GROUP):
            head_out[i][hk * GROUP + g] = o[g * ATTN_BLOCK:(g + 1) * ATTN_BLOCK, :]

    tiles = {}

    def proj_slice(lo):
        tiles[lo] = proj(lo, lo + MXU_TILE)

    def gather(lo0, hi0):
        return jnp.concatenate([tiles[lo] for lo in range(lo0, hi0, MXU_TILE)], axis=1)

    def spatial_mixing():
        h_vg = jnp.concatenate([ahead[:, QKV_WIDTH:], gather(VG0 + AHEAD_VG, ZG0)], axis=1)
        vn = _layer_norm(_gelu(h_vg), gln_g_ref[...], gln_b_ref[...]).astype(BF16)
        causal = (lax.broadcasted_iota(jnp.int32, (CHUNK, CHUNK), 0)
                  >= lax.broadcasted_iota(jnp.int32, (CHUNK, CHUNK), 1))
        mixed_cols = []
        for grp in range(GMLP_GROUPS):
            c0 = grp * GMLP_GROUP_DIM
            w = jnp.where(causal, ws_ref[grp], 0.0).astype(BF16)
            rhs = jnp.concatenate([vn[c * CHUNK:(c + 1) * CHUNK, c0:c0 + GMLP_GROUP_DIM]
                                   for c in range(CHUNKS_PER_STEP)], axis=1)
            mix = jnp.dot(w, rhs, preferred_element_type=F32) + bs_ref[grp]
            mixed_cols.append(jnp.concatenate(
                [mix[:, c * GMLP_GROUP_DIM:(c + 1) * GMLP_GROUP_DIM]
                 for c in range(CHUNKS_PER_STEP)], axis=0))
        tiles["mixed"] = jnp.concatenate(mixed_cols, axis=1)

    order = [VG0 + AHEAD_VG, U0, U0 + MXU_TILE, ZG0, None, ZG0 + MXU_TILE, ZA0, ZA0 + MXU_TILE]
    order += list(range(GA0, GG0, MXU_TILE))
    fillers = [spatial_mixing if lo is None else functools.partial(proj_slice, lo) for lo in order]
    fillers.reverse()

    in_flight = []
    for unit in [(hk, i) for i in range(BLOCKS_PER_STEP) for hk in range(N_KV_HEADS)]:
        in_flight.append(unit + attn_probs(*unit))
        fillers.pop()()
        if len(in_flight) > 2:
            attn_values(*in_flight.pop(0))
    for unit in in_flight:
        fillers.pop()()
        attn_values(*unit)
    while fillers:
        fillers.pop()()

    h_u, h_zg, h_za = gather(U0, VG0), gather(ZG0, GA0), gather(ZA0, U0)
    h_ga = gather(GA0, GG0)
    mixed = tiles["mixed"]

    h_gg = proj(GG0, _END)

    att = jnp.concatenate([jnp.concatenate(head_out[i], axis=1) for i in range(BLOCKS_PER_STEP)],
                          axis=0)
    y_a = (att * _silu(h_za)).astype(BF16)
    y_g = (_gelu(h_u) * mixed * _silu(h_zg)).astype(BF16)

    br_a = jnp.dot(y_a, w_ba_ref[...], preferred_element_type=F32)
    br_g = jnp.dot(y_g, w_bg_ref[...], preferred_element_type=F32)
    merged = jax.nn.sigmoid(h_ga) * br_a + jax.nn.sigmoid(h_gg) * br_g
    out = jnp.dot(merged.astype(BF16), w_out_ref[...], preferred_element_type=F32) + b_out_ref[...]
    o_ref[...] = _layer_norm(ALPHA * x + out, ln_g_ref[...], ln_b_ref[...])

    xb_next = x_next_ref[...].astype(BF16)
    xb_ref[write_slot] = xb_next
    ahead_ref[write_slot] = _project_ahead(xb_next, w_in_ref, b_in_ref)


def _const_spec(shape, layer):
    zeros = (0,) * len(shape)
    return pl.BlockSpec((None,) + tuple(shape), lambda t: (layer,) + zeros,
                        pipeline_mode=pl.Buffered(1))


def _layer_call(layer, x, params):
    batch, seq, _ = x.shape
    assert seq % TOKENS_PER_STEP == 0
    tiles_per_seq = seq // TOKENS_PER_STEP
    n_tiles = batch * tiles_per_seq
    (sinks, w_in, b_in, gln_g, gln_b, ws, bs, w_ba, w_bg, w_out, b_out, ln_g, ln_b) = params

    def tile_block(tile):
        return (tile // tiles_per_seq, tile % tiles_per_seq, 0)

    block = (None, TOKENS_PER_STEP, D_MODEL)
    x_spec = pl.BlockSpec(block, lambda t: tile_block(t))
    x_next_spec = pl.BlockSpec(block, lambda t: tile_block(jnp.minimum(t + 1, n_tiles - 1)))
    in_specs = [
        pl.BlockSpec(memory_space=pltpu.SMEM),
        x_spec,
        x_next_spec,
        pl.BlockSpec(memory_space=pl.ANY),
        _const_spec((1, IN_WIDTH), layer),
        _const_spec((1, GMLP_WIDTH), layer),
        _const_spec((1, GMLP_WIDTH), layer),
        _const_spec((GMLP_GROUPS, CHUNK, CHUNK), layer),
        _const_spec((GMLP_GROUPS, CHUNK, 1), layer),
        pl.BlockSpec(memory_space=pl.ANY),
        pl.BlockSpec(memory_space=pl.ANY),
        pl.BlockSpec(memory_space=pl.ANY),
        _const_spec((1, D_MODEL), layer),
        _const_spec((1, D_MODEL), layer),
        _const_spec((1, D_MODEL), layer),
    ]
    return pl.pallas_call(
        functools.partial(_layer_kernel, layer=layer, tiles_per_seq=tiles_per_seq),
        grid=(n_tiles,),
        in_specs=in_specs,
        out_specs=x_spec,
        out_shape=jax.ShapeDtypeStruct(x.shape, x.dtype),
        scratch_shapes=[
            pltpu.VMEM((2, ATTN_BLOCK, 2 * KV_WIDTH), BF16),
            pltpu.VMEM((2, N_KV_HEADS, GROUP * ATTN_BLOCK, 2 * ATTN_BLOCK), F32),
            pltpu.VMEM((2, TOKENS_PER_STEP, AHEAD_WIDTH), F32),
            pltpu.VMEM((2, TOKENS_PER_STEP, D_MODEL), BF16),
            pltpu.VMEM((D_MODEL, IN_WIDTH), BF16),
            pltpu.VMEM((ATTN_WIDTH, D_MODEL), BF16),
            pltpu.VMEM((GMLP_WIDTH, D_MODEL), BF16),
            pltpu.VMEM((D_MODEL, D_MODEL), BF16),
            pltpu.VMEM((WEIGHT_STAGE_SLOTS, WIDE_CHUNK_ROWS, IN_WIDTH), F32),
            pltpu.SemaphoreType.DMA((WEIGHT_STAGE_SLOTS,)),
            pltpu.VMEM((WEIGHT_STAGE_SLOTS, NARROW_CHUNK_ROWS, D_MODEL), F32),
            pltpu.SemaphoreType.DMA((WEIGHT_STAGE_SLOTS,)),
        ],
        compiler_params=pltpu.CompilerParams(
            dimension_semantics=("arbitrary",),
            vmem_limit_bytes=VMEM_LIMIT_BYTES),
        name=f"trunk_layer_{layer}",
    )(sinks[layer], x, x, w_in, b_in, gln_g, gln_b, ws, bs, w_ba, w_bg, w_out, b_out, ln_g, ln_b)


def kernel(x, w_in, b_in, attn_sinks, gmlp_ln_g, gmlp_ln_b, w_spatial, b_spatial,
           w_branch_attn, w_branch_gmlp, w_out, b_out, ln_g, ln_b):
    params = (
        attn_sinks,
        w_in,
        b_in[:, None, :],
        gmlp_ln_g[:, None, :],
        gmlp_ln_b[:, None, :],
        w_spatial,
        b_spatial[..., None],
        w_branch_attn,
        w_branch_gmlp,
        w_out,
        b_out[:, None, :],
        ln_g[:, None, :],
        ln_b[:, None, :],
    )
    for layer in range(DEPTH):
        x = _layer_call(layer, x, params)
    return x
```
